```python
import jax, jax.numpy as jnp
from jax import lax
import numpy as np

D_MODEL = 4096
BATCH = 2
SEQ = 8192
DEPTH = 4

BLOCK = 128
EPS = 1e-6
NEG = -1e30
A_GROUPS = ((128, 1), (512, 4), (2048, 16))
A_HEADS = 6
A_HEAD_DIM = 128
A_WIDTH = A_HEADS * A_HEAD_DIM
A_QKV = len(A_GROUPS) * A_WIDTH
B_Q_HEADS = 16
B_KV_HEADS = 2
B_HEAD_DIM = 64
B_WINDOW = 128
B_WIDTH = B_Q_HEADS * B_HEAD_DIM
B_KV = B_KV_HEADS * B_HEAD_DIM
C_HEADS = 4
C_QK_DIM = 128
C_V_DIM = 256
C_QK = C_HEADS * C_QK_DIM
C_WIDTH = C_HEADS * C_V_DIM
C_CHUNK = 128
ROT_BASE = 10000.0
GATE_RANK = 256
N_BRANCH = 3
MIX_WIDTH = A_WIDTH + B_WIDTH + C_WIDTH
IN_WIDTH = 3 * A_QKV + B_WIDTH + 2 * B_KV + 2 * C_QK + 2 * C_WIDTH + GATE_RANK
D_FF = 4 * D_MODEL

kernel_name = "hybrid_dilated_swa_retention_block"


def rmsnorm(x, g):
    xf = x.astype(jnp.float32)
    y = xf * lax.rsqrt(jnp.mean(xf * xf, axis=-1, keepdims=True) + EPS)
    return (y * g.astype(jnp.float32)).astype(x.dtype)


def banded_attention(q, k, v, max_dist, sinks=None):
    n, L, hq, dh = q.shape
    hk = k.shape[2]
    grp = hq // hk
    nb = -(-L // BLOCK)
    padw = ((0, 0), (0, nb * BLOCK - L), (0, 0), (0, 0))
    qb = jnp.pad(q, padw).astype(jnp.float32).reshape(n, nb, BLOCK, hk, grp, dh)
    kb = jnp.pad(k, padw).astype(jnp.float32).reshape(n, nb, BLOCK, hk, dh)
    vb = jnp.pad(v, padw).astype(jnp.float32).reshape(n, nb, BLOCK, hk, dh)
    prev = lambda t: jnp.pad(t, ((0, 0), (1, 0), (0, 0), (0, 0), (0, 0)))[:, :-1]
    kw = jnp.concatenate([prev(kb), kb], axis=2)
    vw = jnp.concatenate([prev(vb), vb], axis=2)
    s = jnp.einsum('nbqhgd,nbkhd->nbhgqk', qb, kw) * (dh ** -0.5)
    qi = jnp.arange(BLOCK)[:, None]
    kj = jnp.arange(2 * BLOCK)[None, :]
    dist = BLOCK + qi - kj
    band = (dist >= 0) & (dist <= max_dist)
    valid = band[None] & ((jnp.arange(nb)[:, None, None] > 0) | (kj >= BLOCK)[None])
    s = jnp.where(valid[None, :, None, None], s, NEG)
    m = s.max(-1)
    if sinks is not None:
        sk = sinks.astype(jnp.float32).reshape(hk, grp)[None, None, :, :, None]
        m = jnp.maximum(m, sk)
    p = jnp.exp(s - m[..., None])
    den = p.sum(-1)
    if sinks is not None:
        den = den + jnp.exp(sk - m)
    o = jnp.einsum('nbhgqk,nbkhd->nbqhgd', p, vw)
    den_t = den.transpose(0, 1, 4, 2, 3)
    o = (o / den_t[..., None]).reshape(n, nb * BLOCK, hq, dh)[:, :L]
    lse = (m.transpose(0, 1, 4, 2, 3) + jnp.log(den_t)).reshape(n, nb * BLOCK, hq)[:, :L]
    return o, lse


def to_residues(t, r):
    b, s = t.shape[:2]
    t = jnp.swapaxes(t.reshape(b, s // r, r, *t.shape[2:]), 1, 2)
    return t.reshape(b * r, s // r, *t.shape[3:])


def from_residues(t, b):
    br, L = t.shape[:2]
    r = br // b
    t = jnp.swapaxes(t.reshape(b, r, L, *t.shape[2:]), 1, 2)
    return t.reshape(b, L * r, *t.shape[3:])


def dilated_attention(q, k, v):
    b = q.shape[0]
    outs, lses = [], []
    for g, (w, r) in enumerate(A_GROUPS):
        o, lse = banded_attention(to_residues(q[:, :, g], r), to_residues(k[:, :, g], r),
                                  to_residues(v[:, :, g], r), w // r)
        outs.append(from_residues(o, b))
        lses.append(from_residues(lse, b))
    o = jnp.stack(outs, axis=2)
    wgt = jax.nn.softmax(jnp.stack(lses, axis=2), axis=2)
    return jnp.einsum('bsghd,bsgh->bshd', o, wgt)


def xpos_rotate(t, cos, sin):
    t1 = t[..., ::2]
    t2 = t[..., 1::2]
    rot = jnp.stack([-t2, t1], axis=-1).reshape(t.shape)
    return t * cos[:, None, :] + rot * sin[:, None, :]


def retention(q, k, v):
    b, s, h, dk = q.shape
    dv = v.shape[-1]
    nc = s // C_CHUNK
    log_g = jnp.log1p(-(2.0 ** (-5.0 - jnp.arange(h, dtype=jnp.float32))))
    qc = q.astype(jnp.float32).reshape(b, nc, C_CHUNK, h, dk)
    kc = k.astype(jnp.float32).reshape(b, nc, C_CHUNK, h, dk)
    vc = v.astype(jnp.float32).reshape(b, nc, C_CHUNK, h, dv)
    idx = jnp.arange(C_CHUNK, dtype=jnp.float32)
    rel = idx[:, None] - idx[None, :]
    decay = jnp.where(rel >= 0, jnp.exp(log_g[:, None, None] * jnp.maximum(rel, 0.0)), 0.0)
    inner = jnp.einsum('bcihd,bcjhd->bchij', qc, kc) * decay
    o_in = jnp.einsum('bchij,bcjhe->bcihe', inner, vc)
    k_dec = kc * jnp.exp(log_g * (C_CHUNK - 1 - idx)[:, None])[..., None]
    kv = jnp.einsum('bcjhd,bcjhe->bchde', k_dec, vc)
    chunk_decay = jnp.exp(log_g * C_CHUNK)[:, None, None]

    def step(state, kv_c):
        return chunk_decay * state + kv_c, state

    _, prev = lax.scan(step, jnp.zeros((b, h, dk, dv), jnp.float32), jnp.moveaxis(kv, 1, 0))
    prev = jnp.moveaxis(prev, 0, 1)
    q_dec = qc * jnp.exp(log_g * (idx + 1.0)[:, None])[..., None]
    o_x = jnp.einsum('bcihd,bchde->bcihe', q_dec, prev)
    return (o_in + o_x).reshape(b, s, h, dv)


def hybrid_layer(x, norm1_g, w_in, qn_a, kn_a, qn_b, kn_b, sinks, w_branch, w_gate_up, b_gate,
                 w_out, norm2_g, w_ff1, w_ff2, cos, sin):
    b, s, _ = x.shape
    dt = x.dtype
    hn = rmsnorm(x, norm1_g)
    proj = jnp.einsum('bsd,de->bse', hn, w_in)
    sizes = (A_QKV, A_QKV, A_QKV, B_WIDTH, B_KV, B_KV, C_QK, C_QK, C_WIDTH, C_WIDTH, GATE_RANK)
    cuts = np.cumsum(sizes)[:-1].tolist()
    qa, ka, va, qb, kb, vb, qc, kc, vc, gc, gl = jnp.split(proj, cuts, axis=-1)

    shp_a = (b, s, len(A_GROUPS), A_HEADS, A_HEAD_DIM)
    o_a = dilated_attention(rmsnorm(qa.reshape(shp_a), qn_a), rmsnorm(ka.reshape(shp_a), kn_a),
                            va.reshape(shp_a)).reshape(b, s, A_WIDTH)

    qb = rmsnorm(qb.reshape(b, s, B_Q_HEADS, B_HEAD_DIM), qn_b)
    kb = rmsnorm(kb.reshape(b, s, B_KV_HEADS, B_HEAD_DIM), kn_b)
    vb = vb.reshape(b, s, B_KV_HEADS, B_HEAD_DIM)
    o_b, _ = banded_attention(qb, kb, vb, B_WINDOW - 1, sinks)
    o_b = o_b.reshape(b, s, B_WIDTH)

    qc = xpos_rotate(qc.reshape(b, s, C_HEADS, C_QK_DIM), cos, sin)
    kc = xpos_rotate(kc.reshape(b, s, C_HEADS, C_QK_DIM), cos, sin) * (C_QK_DIM ** -0.5)
    r = retention(qc, kc, vc.reshape(b, s, C_HEADS, C_V_DIM))
    r = r * lax.rsqrt(jnp.mean(r * r, axis=-1, keepdims=True) + EPS)
    o_c = r.reshape(b, s, C_WIDTH) * jax.nn.silu(gc.astype(jnp.float32))

    w_rows = jnp.split(w_branch, [A_WIDTH, A_WIDTH + B_WIDTH], axis=0)
    mixed = jnp.zeros((b, s, D_MODEL), jnp.float32)
    for i, o_br in enumerate((o_a, o_b, o_c)):
        gate = jax.nn.sigmoid((jnp.einsum('bsr,re->bse', gl, w_gate_up[:, i * D_MODEL:(i + 1) * D_MODEL])
                               + b_gate[i * D_MODEL:(i + 1) * D_MODEL]).astype(jnp.float32))
        mixed = mixed + gate * jnp.einsum('bse,ed->bsd', o_br.astype(dt), w_rows[i]).astype(jnp.float32)
    x = x + jnp.einsum('bsd,de->bse', mixed.astype(dt), w_out)

    hn = rmsnorm(x, norm2_g)
    hid = jnp.square(jax.nn.relu(jnp.einsum('bsd,df->bsf', hn, w_ff1)))
    return x + jnp.einsum('bsf,fd->bsd', hid, w_ff2)


def setup_inputs(seed: int = 0) -> dict:
    key = jax.random.key(seed)
    ks = jax.random.split(key, 20)
    f = jnp.float32
    nrm = lambda k, shape, scale: jax.random.normal(k, shape, f) * scale
    w_branch = jnp.concatenate([
        nrm(ks[9], (DEPTH, A_WIDTH, D_MODEL), A_WIDTH ** -0.5),
        nrm(ks[10], (DEPTH, B_WIDTH, D_MODEL), B_WIDTH ** -0.5),
        nrm(ks[11], (DEPTH, C_WIDTH, D_MODEL), C_WIDTH ** -0.5)], axis=1)
    return {
        "x": nrm(ks[0], (BATCH, SEQ, D_MODEL), 1.0),
        "norm1_g": 1.0 + nrm(ks[1], (DEPTH, D_MODEL), 0.02),
        "w_in": nrm(ks[2], (DEPTH, D_MODEL, IN_WIDTH), D_MODEL ** -0.5),
        "qn_a": 1.0 + nrm(ks[3], (DEPTH, A_HEAD_DIM), 0.02),
        "kn_a": 1.0 + nrm(ks[4], (DEPTH, A_HEAD_DIM), 0.02),
        "qn_b": 1.0 + nrm(ks[5], (DEPTH, B_HEAD_DIM), 0.02),
        "kn_b": 1.0 + nrm(ks[6], (DEPTH, B_HEAD_DIM), 0.02),
        "sinks": nrm(ks[7], (DEPTH, B_Q_HEADS), 0.5),
        "w_branch": w_branch,
        "w_gate_up": nrm(ks[12], (DEPTH, GATE_RANK, N_BRANCH * D_MODEL), GATE_RANK ** -0.5),
        "b_gate": nrm(ks[13], (DEPTH, N_BRANCH * D_MODEL), 0.1),
        "w_out": nrm(ks[14], (DEPTH, D_MODEL, D_MODEL), D_MODEL ** -0.5),
        "norm2_g": 1.0 + nrm(ks[15], (DEPTH, D_MODEL), 0.02),
        "w_ff1": nrm(ks[16], (DEPTH, D_MODEL, D_FF), D_MODEL ** -0.5),
        "w_ff2": nrm(ks[17], (DEPTH, D_FF, D_MODEL), D_FF ** -0.5),
    }


def reference(x, norm1_g, w_in, qn_a, kn_a, qn_b, kn_b, sinks, w_branch, w_gate_up, b_gate,
              w_out, norm2_g, w_ff1, w_ff2):
    s = x.shape[1]
    pos = jnp.arange(s, dtype=jnp.float32)
    inv = 1.0 / (ROT_BASE ** jnp.linspace(0.0, 1.0, C_QK_DIM // 2, dtype=jnp.float32))
    ang = pos[:, None] * jnp.repeat(inv, 2)[None, :]
    cos, sin = jnp.cos(ang), jnp.sin(ang)
    for i in range(DEPTH):
        x = hybrid_layer(x, norm1_g[i], w_in[i], qn_a[i], kn_a[i], qn_b[i], kn_b[i], sinks[i],
                         w_branch[i], w_gate_up[i], b_gate[i], w_out[i], norm2_g[i],
                         w_ff1[i], w_ff2[i], cos, sin)
    return x
```

```python
import functools
import math

import jax
import jax.numpy as jnp
from jax import lax
from jax.experimental import pallas as pl
from jax.experimental.pallas import tpu as pltpu

F32 = jnp.float32
BF16 = jnp.bfloat16

EPS = 1e-6
NEG = -1e30
BLOCK = 128
A_GROUPS = ((128, 1), (512, 4), (2048, 16))
A_HEADS = 6
A_HEAD_DIM = 128
A_WIDTH = A_HEADS * A_HEAD_DIM
A_QKV = len(A_GROUPS) * A_WIDTH
B_Q_HEADS = 16
B_KV_HEADS = 2
B_HEAD_DIM = 64
B_WINDOW = 128
B_WIDTH = B_Q_HEADS * B_HEAD_DIM
B_KV = B_KV_HEADS * B_HEAD_DIM
C_HEADS = 4
C_QK_DIM = 128
C_V_DIM = 256
C_QK = C_HEADS * C_QK_DIM
C_WIDTH = C_HEADS * C_V_DIM
C_CHUNK = 128
ROT_BASE = 10000.0
GATE_RANK = 256
N_BRANCH = 3

OFF_QA = 0
OFF_KA = OFF_QA + A_QKV
OFF_VA = OFF_KA + A_QKV
OFF_QB = OFF_VA + A_QKV
OFF_KB = OFF_QB + B_WIDTH
OFF_VB = OFF_KB + B_KV
OFF_QC = OFF_VB + B_KV
OFF_KC = OFF_QC + C_QK
OFF_VC = OFF_KC + C_QK
OFF_GC = OFF_VC + C_WIDTH
OFF_GL = OFF_GC + C_WIDTH
IN_WIDTH = OFF_GL + GATE_RANK

V7X_VMEM_BYTES = 64 * 1024 * 1024
VMEM_LIMIT_BYTES = V7X_VMEM_BYTES * 7 // 8
LANES = 128


def _tile(dim, pref):
    t = min(dim, pref)
    while dim % t:
        t -= LANES
    return t


def _params(*sem):
    return pltpu.CompilerParams(dimension_semantics=sem, vmem_limit_bytes=VMEM_LIMIT_BYTES)


def _rms_rows(t, g, n):
    ms = jnp.sum(t * t, axis=-1, keepdims=True) * (1.0 / n)
    return t * lax.rsqrt(ms + EPS) * g


def _dot_nt(a, b):
    return lax.dot_general(a, b, (((1,), (1,)), ((), ())), preferred_element_type=F32)


def _dot_tn(a, b):
    return lax.dot_general(a, b, (((0,), (0,)), ((), ())), preferred_element_type=F32)


def _sigmoid(z):
    return 1.0 / (1.0 + jnp.exp(-z))


NORM_ROWS = 32


def _mm_norm_kernel(x_ref, g_ref, w_ref, o_ref, hn_ref, *, relu2):
    @pl.when(pl.program_id(1) == 0)
    def _():
        g = g_ref[...]
        n = x_ref.shape[1]

        def body(c, carry):
            r = pl.multiple_of(c * NORM_ROWS, NORM_ROWS)
            x = x_ref[pl.ds(r, NORM_ROWS), :]
            hn_ref[pl.ds(r, NORM_ROWS), :] = _rms_rows(x, g, n).astype(BF16)
            return carry

        lax.fori_loop(0, x_ref.shape[0] // NORM_ROWS, body, 0)

    acc = jnp.dot(hn_ref[...], w_ref[...], preferred_element_type=F32)
    if relu2:
        acc = jnp.square(jnp.maximum(acc, 0.0))
    o_ref[...] = acc.astype(o_ref.dtype)


def _mm_norm(x, g, w, *, relu2, tm, tn):
    t, k = x.shape
    n = w.shape[1]
    tm = _tile(t, tm)
    tn = _tile(n, tn)
    return pl.pallas_call(
        functools.partial(_mm_norm_kernel, relu2=relu2),
        grid=(t // tm, n // tn),
        in_specs=[
            pl.BlockSpec((tm, k), lambda i, j: (i, 0)),
            pl.BlockSpec((1, k), lambda i, j: (0, 0)),
            pl.BlockSpec((k, tn), lambda i, j: (0, j)),
        ],
        out_specs=pl.BlockSpec((tm, tn), lambda i, j: (i, j)),
        out_shape=jax.ShapeDtypeStruct((t, n), BF16),
        scratch_shapes=[pltpu.VMEM((tm, k), BF16)],
        compiler_params=_params("parallel", "arbitrary"),
        name="mm_norm_relu2" if relu2 else "mm_norm",
    )(x, g.reshape(1, k), w)


def _mm_res_kernel(a_ref, w_ref, r_ref, o_ref):
    part = jnp.dot(a_ref[...], w_ref[...], preferred_element_type=F32)

    @pl.when(pl.program_id(2) == 0)
    def _():
        o_ref[...] = r_ref[...] + part

    @pl.when(pl.program_id(2) > 0)
    def _():
        o_ref[...] += part


def _mm_res(a, w, res, *, tm, tn, tk):
    t, k = a.shape
    n = w.shape[1]
    tm = _tile(t, tm)
    tn = _tile(n, tn)
    tk = _tile(k, tk)
    return pl.pallas_call(
        _mm_res_kernel,
        grid=(t // tm, n // tn, k // tk),
        in_specs=[
            pl.BlockSpec((tm, tk), lambda i, j, l: (i, l)),
            pl.BlockSpec((tk, tn), lambda i, j, l: (l, j)),
            pl.BlockSpec((tm, tn), lambda i, j, l: (i, j)),
        ],
        out_specs=pl.BlockSpec((tm, tn), lambda i, j, l: (i, j)),
        out_shape=jax.ShapeDtypeStruct((t, n), F32),
        compiler_params=_params("parallel", "parallel", "arbitrary"),
        name="mm_res",
    )(a, w, res)


def _attn_a_kernel(q_ref, kp_ref, kc_ref, vp_ref, vc_ref, gq_ref, gk_ref, o_ref, l_ref, *, max_dist):
    j = pl.program_id(2)
    qi = lax.broadcasted_iota(jnp.int32, (BLOCK, BLOCK), 0)
    kj = lax.broadcasted_iota(jnp.int32, (BLOCK, BLOCK), 1)
    valid_c = kj <= qi
    first = jnp.where(j > 0, BLOCK - max_dist, BLOCK)
    valid_p = kj >= qi + first
    gq = gq_ref[...]
    gk = gk_ref[...]
    scale = A_HEAD_DIM ** -0.5
    for h in range(A_HEADS):
        sl = slice(h * A_HEAD_DIM, (h + 1) * A_HEAD_DIM)
        q = _rms_rows(q_ref[:, sl].astype(F32), gq, A_HEAD_DIM).astype(BF16)
        kp = _rms_rows(kp_ref[:, sl].astype(F32), gk, A_HEAD_DIM).astype(BF16)
        kc = _rms_rows(kc_ref[:, sl].astype(F32), gk, A_HEAD_DIM).astype(BF16)
        s_p = jnp.where(valid_p, _dot_nt(q, kp) * scale, NEG)
        s_c = jnp.where(valid_c, _dot_nt(q, kc) * scale, NEG)
        m = jnp.maximum(jnp.max(s_p, axis=-1, keepdims=True), jnp.max(s_c, axis=-1, keepdims=True))
        p_p = jnp.exp(s_p - m)
        p_c = jnp.exp(s_c - m)
        den = jnp.sum(p_p, axis=-1, keepdims=True) + jnp.sum(p_c, axis=-1, keepdims=True)
        acc = (jnp.dot(p_p.astype(BF16), vp_ref[:, sl], preferred_element_type=F32)
               + jnp.dot(p_c.astype(BF16), vc_ref[:, sl], preferred_element_type=F32))
        o_ref[:, sl] = acc / den
        l_ref[:, sl] = jnp.broadcast_to(m + jnp.log(den), (BLOCK, A_HEAD_DIM))


def _attn_a_group(proj, gq, gk, g, batch):
    w, r = A_GROUPS[g]
    t = proj.shape[0]
    nb = t // (batch * r * BLOCK)
    cb = IN_WIDTH // A_WIDTH
    pv = proj.reshape(t // r, r * IN_WIDTH)
    row = lambda b, c, j: b * nb + j
    prev = lambda b, c, j: b * nb + jnp.maximum(j - 1, 0)
    qcol, kcol, vcol = g, len(A_GROUPS) + g, 2 * len(A_GROUPS) + g
    blk = (BLOCK, A_WIDTH)
    out = jax.ShapeDtypeStruct((t // r, r * A_WIDTH), F32)
    o, l = pl.pallas_call(
        functools.partial(_attn_a_kernel, max_dist=w // r),
        grid=(batch, r, nb),
        in_specs=[
            pl.BlockSpec(blk, lambda b, c, j: (row(b, c, j), c * cb + qcol)),
            pl.BlockSpec(blk, lambda b, c, j: (prev(b, c, j), c * cb + kcol)),
            pl.BlockSpec(blk, lambda b, c, j: (row(b, c, j), c * cb + kcol)),
            pl.BlockSpec(blk, lambda b, c, j: (prev(b, c, j), c * cb + vcol)),
            pl.BlockSpec(blk, lambda b, c, j: (row(b, c, j), c * cb + vcol)),
            pl.BlockSpec((1, A_HEAD_DIM), lambda b, c, j: (0, 0)),
            pl.BlockSpec((1, A_HEAD_DIM), lambda b, c, j: (0, 0)),
        ],
        out_specs=[
            pl.BlockSpec(blk, lambda b, c, j: (row(b, c, j), c)),
            pl.BlockSpec(blk, lambda b, c, j: (row(b, c, j), c)),
        ],
        out_shape=[out, out],
        compiler_params=_params("parallel", "parallel", "arbitrary"),
        name=f"attn_a_r{r}",
    )(pv, pv, pv, pv, pv, gq, gk)
    return o.reshape(t, A_WIDTH), l.reshape(t, A_WIDTH)


def _mix_a_kernel(o0_ref, o1_ref, o2_ref, l0_ref, l1_ref, l2_ref, o_ref):
    l0, l1, l2 = l0_ref[...], l1_ref[...], l2_ref[...]
    m = jnp.maximum(jnp.maximum(l0, l1), l2)
    e0, e1, e2 = jnp.exp(l0 - m), jnp.exp(l1 - m), jnp.exp(l2 - m)
    num = e0 * o0_ref[...] + e1 * o1_ref[...] + e2 * o2_ref[...]
    o_ref[...] = (num / (e0 + e1 + e2)).astype(BF16)


def _mix_a(os_, ls_, *, tm):
    t = os_[0].shape[0]
    tm = _tile(t, tm)
    spec = pl.BlockSpec((tm, A_WIDTH), lambda i: (i, 0))
    return pl.pallas_call(
        _mix_a_kernel,
        grid=(t // tm,),
        in_specs=[spec] * 6,
        out_specs=spec,
        out_shape=jax.ShapeDtypeStruct((t, A_WIDTH), BF16),
        compiler_params=_params("parallel"),
        name="mix_a",
    )(*os_, *ls_)


HALF = B_HEAD_DIM


def _attn_b_kernel(sink_ref, q0_ref, q1_ref, q2_ref, q3_ref, kp_ref, kc_ref, vp_ref, vc_ref,
                   gq_ref, gk_ref, o_ref):
    j = pl.program_id(1)
    qi = lax.broadcasted_iota(jnp.int32, (BLOCK, BLOCK), 0)
    kj = lax.broadcasted_iota(jnp.int32, (BLOCK, BLOCK), 1)
    low = kj < HALF
    valid_c = kj <= qi
    first = jnp.where(j > 0, BLOCK - (B_WINDOW - 1), BLOCK)
    valid_p = kj >= qi + first
    gq = gq_ref[...]
    gk = gk_ref[...]
    scale = B_HEAD_DIM ** -0.5

    def norm_halves(t, g):
        sq = t * t
        s_lo = jnp.sum(jnp.where(low, sq, 0.0), axis=-1, keepdims=True) * (1.0 / HALF)
        s_hi = jnp.sum(jnp.where(low, 0.0, sq), axis=-1, keepdims=True) * (1.0 / HALF)
        return t * jnp.where(low, lax.rsqrt(s_lo + EPS), lax.rsqrt(s_hi + EPS)) * g

    def placements(t):
        h0_lo = jnp.where(low, t, 0.0)
        h1_hi = jnp.where(low, 0.0, t)
        h0_hi = pltpu.roll(h0_lo, HALF, 1)
        h1_lo = pltpu.roll(h1_hi, HALF, 1)
        return {(0, 0): h0_lo.astype(BF16), (0, 1): h0_hi.astype(BF16),
                (1, 0): h1_lo.astype(BF16), (1, 1): h1_hi.astype(BF16)}

    kp = placements(norm_halves(kp_ref[...].astype(F32), gk))
    kc = placements(norm_halves(kc_ref[...].astype(F32), gk))
    vp = placements(vp_ref[...].astype(F32))
    vc = placements(vc_ref[...].astype(F32))

    q_refs = (q0_ref, q1_ref, q2_ref, q3_ref)
    tiles = B_WIDTH // LANES
    per_kv = tiles // B_KV_HEADS
    for a in range(tiles):
        kv = a // per_kv
        q = q_refs[a // 2][:, (a % 2) * LANES:(a % 2 + 1) * LANES].astype(F32)
        q = norm_halves(q, gq).astype(BF16)
        out = None
        for half in range(2):
            sink = sink_ref[2 * a + half]
            s_p = jnp.where(valid_p, _dot_nt(q, kp[kv, half]) * scale, NEG)
            s_c = jnp.where(valid_c, _dot_nt(q, kc[kv, half]) * scale, NEG)
            m = jnp.maximum(jnp.max(s_p, axis=-1, keepdims=True), jnp.max(s_c, axis=-1, keepdims=True))
            m = jnp.maximum(m, sink)
            p_p = jnp.exp(s_p - m)
            p_c = jnp.exp(s_c - m)
            den = (jnp.sum(p_p, axis=-1, keepdims=True) + jnp.sum(p_c, axis=-1, keepdims=True)
                   + jnp.exp(sink - m))
            acc = (jnp.dot(p_p.astype(BF16), vp[kv, half], preferred_element_type=F32)
                   + jnp.dot(p_c.astype(BF16), vc[kv, half], preferred_element_type=F32))
            part = acc / den
            out = part if out is None else out + part
        o_ref[:, a * LANES:(a + 1) * LANES] = out.astype(BF16)


def _attn_b(proj, gq, gk, sinks, batch):
    t = proj.shape[0]
    nb = t // (batch * BLOCK)
    row = lambda b, j: b * nb + j
    prev = lambda b, j: b * nb + jnp.maximum(j - 1, 0)
    qw = 2 * LANES
    q_specs = [pl.BlockSpec((BLOCK, qw), functools.partial(lambda b, j, u: (row(b, j), OFF_QB // qw + u), u=u))
               for u in range(B_WIDTH // qw)]
    kv_blk = (BLOCK, B_KV)
    return pl.pallas_call(
        _attn_b_kernel,
        grid=(batch, nb),
        in_specs=[pl.BlockSpec(memory_space=pltpu.SMEM)] + q_specs + [
            pl.BlockSpec(kv_blk, lambda b, j: (prev(b, j), OFF_KB // B_KV)),
            pl.BlockSpec(kv_blk, lambda b, j: (row(b, j), OFF_KB // B_KV)),
            pl.BlockSpec(kv_blk, lambda b, j: (prev(b, j), OFF_VB // B_KV)),
            pl.BlockSpec(kv_blk, lambda b, j: (row(b, j), OFF_VB // B_KV)),
            pl.BlockSpec((1, LANES), lambda b, j: (0, 0)),
            pl.BlockSpec((1, LANES), lambda b, j: (0, 0)),
        ],
        out_specs=pl.BlockSpec((BLOCK, B_WIDTH), lambda b, j: (row(b, j), 0)),
        out_shape=jax.ShapeDtypeStruct((t, B_WIDTH), BF16),
        compiler_params=_params("parallel", "arbitrary"),
        name="attn_b",
    )(sinks, proj, proj, proj, proj, proj, proj, proj, proj, gq, gk)


def _ret_kernel(q_ref, k_ref, v_ref, g_ref, cos_ref, sa_ref, sb_ref, o_ref, st_ref):
    @pl.when(pl.program_id(1) == 0)
    def _():
        st_ref[...] = jnp.zeros(st_ref.shape, F32)

    cos, sa, sb = cos_ref[...], sa_ref[...], sb_ref[...]
    ri = lax.broadcasted_iota(jnp.int32, (C_CHUNK, C_CHUNK), 0)
    ci = lax.broadcasted_iota(jnp.int32, (C_CHUNK, C_CHUNK), 1)
    rel = (ri - ci).astype(F32)
    idx = lax.broadcasted_iota(jnp.int32, (C_CHUNK, 1), 0).astype(F32)

    def rotate(t):
        return t * cos + pltpu.roll(t, C_QK_DIM - 1, 1) * sa + pltpu.roll(t, 1, 1) * sb

    for h in range(C_HEADS):
        log_g = math.log1p(-(2.0 ** (-5.0 - h)))
        qs = slice(h * C_QK_DIM, (h + 1) * C_QK_DIM)
        vs = slice(h * C_V_DIM, (h + 1) * C_V_DIM)
        q = rotate(q_ref[:, qs].astype(F32))
        k = rotate(k_ref[:, qs].astype(F32)) * (C_QK_DIM ** -0.5)
        v = v_ref[:, vs]
        decay = jnp.where(rel >= 0.0, jnp.exp(log_g * jnp.maximum(rel, 0.0)), 0.0)
        inner = _dot_nt(q.astype(BF16), k.astype(BF16)) * decay
        o_in = jnp.dot(inner.astype(BF16), v, preferred_element_type=F32)
        state = st_ref[h]
        q_dec = q * jnp.exp(log_g * (idx + 1.0))
        o_x = jnp.dot(q_dec.astype(BF16), state.astype(BF16), preferred_element_type=F32)
        k_dec = k * jnp.exp(log_g * (C_CHUNK - 1.0 - idx))
        st_ref[h] = math.exp(log_g * C_CHUNK) * state + _dot_tn(k_dec.astype(BF16), v)
        r = o_in + o_x
        r = r * lax.rsqrt(jnp.sum(r * r, axis=-1, keepdims=True) * (1.0 / C_V_DIM) + EPS)
        gate = g_ref[:, vs].astype(F32)
        o_ref[:, vs] = (r * (gate * _sigmoid(gate))).astype(BF16)


def _retention(proj, cos, sa, sb, batch):
    t = proj.shape[0]
    nc = t // (batch * C_CHUNK)
    row = lambda b, j: (b * nc + j)
    tab = pl.BlockSpec((C_CHUNK, C_QK_DIM), lambda b, j: (j, 0))
    return pl.pallas_call(
        _ret_kernel,
        grid=(batch, nc),
        in_specs=[
            pl.BlockSpec((C_CHUNK, C_QK), lambda b, j: (row(b, j), OFF_QC // C_QK)),
            pl.BlockSpec((C_CHUNK, C_QK), lambda b, j: (row(b, j), OFF_KC // C_QK)),
            pl.BlockSpec((C_CHUNK, C_WIDTH), lambda b, j: (row(b, j), OFF_VC // C_WIDTH)),
            pl.BlockSpec((C_CHUNK, C_WIDTH), lambda b, j: (row(b, j), OFF_GC // C_WIDTH)),
            tab, tab, tab,
        ],
        out_specs=pl.BlockSpec((C_CHUNK, C_WIDTH), lambda b, j: (row(b, j), 0)),
        out_shape=jax.ShapeDtypeStruct((t, C_WIDTH), BF16),
        scratch_shapes=[pltpu.VMEM((C_HEADS, C_QK_DIM, C_V_DIM), F32)],
        compiler_params=_params("parallel", "arbitrary"),
        name="retention",
    )(proj, proj, proj, proj, cos, sa, sb)


BRANCH_ROWS = ((0, A_WIDTH), (A_WIDTH, A_WIDTH + B_WIDTH), (A_WIDTH + B_WIDTH, A_WIDTH + B_WIDTH + C_WIDTH))


def _merge_kernel(oa_ref, ob_ref, oc_ref, gl_ref, wb_ref, wg0_ref, wg1_ref, wg2_ref,
                  b0_ref, b1_ref, b2_ref, o_ref):
    gl = gl_ref[...]
    acc = None
    for o_br, (lo, hi), wg_ref, b_ref in zip((oa_ref, ob_ref, oc_ref), BRANCH_ROWS,
                                              (wg0_ref, wg1_ref, wg2_ref), (b0_ref, b1_ref, b2_ref)):
        gate = _sigmoid(jnp.dot(gl, wg_ref[...], preferred_element_type=F32) + b_ref[...])
        term = gate * jnp.dot(o_br[...], wb_ref[lo:hi, :], preferred_element_type=F32)
        acc = term if acc is None else acc + term
    o_ref[...] = acc.astype(BF16)


def _merge(oa, ob, oc, proj, w_branch, w_gate_up, b_gate, *, tm, tn):
    t = oa.shape[0]
    d = w_branch.shape[1]
    tm = _tile(t, tm)
    tn = _tile(d, tn)
    nj = d // tn
    rowblk = lambda w: pl.BlockSpec((tm, w), lambda i, j: (i, 0))
    wg_specs = [pl.BlockSpec((GATE_RANK, tn), functools.partial(lambda i, j, u: (0, u * nj + j), u=u))
                for u in range(N_BRANCH)]
    b_specs = [pl.BlockSpec((1, tn), functools.partial(lambda i, j, u: (0, u * nj + j), u=u))
               for u in range(N_BRANCH)]
    return pl.pallas_call(
        _merge_kernel,
        grid=(t // tm, nj),
        in_specs=[rowblk(A_WIDTH), rowblk(B_WIDTH), rowblk(C_WIDTH),
                  pl.BlockSpec((tm, GATE_RANK), lambda i, j: (i, OFF_GL // GATE_RANK)),
                  pl.BlockSpec((w_branch.shape[0], tn), lambda i, j: (0, j))] + wg_specs + b_specs,
        out_specs=pl.BlockSpec((tm, tn), lambda i, j: (i, j)),
        out_shape=jax.ShapeDtypeStruct((t, d), BF16),
        compiler_params=_params("parallel", "arbitrary"),
        name="merge",
    )(oa, ob, oc, proj, w_branch, w_gate_up, w_gate_up, w_gate_up, b_gate, b_gate, b_gate)


def _rotation_tables(s):
    pos = jnp.arange(s, dtype=F32)
    inv = 1.0 / (ROT_BASE ** jnp.linspace(0.0, 1.0, C_QK_DIM // 2, dtype=F32))
    ang = pos[:, None] * jnp.repeat(inv, 2)[None, :]
    cos, sin = jnp.cos(ang), jnp.sin(ang)
    even = (jnp.arange(C_QK_DIM) % 2 == 0)[None, :]
    return cos, jnp.where(even, -sin, 0.0), jnp.where(even, 0.0, sin)


def kernel(x, norm1_g, w_in, qn_a, kn_a, qn_b, kn_b, sinks, w_branch, w_gate_up, b_gate, w_out, norm2_g, w_ff1, w_ff2):
    batch, s, d = x.shape
    t = batch * s
    depth = w_in.shape[0]
    assert w_in.shape[2] == IN_WIDTH and s % (A_GROUPS[-1][1] * BLOCK) == 0
    cos, sa, sb = _rotation_tables(s)
    w_in, w_branch, w_gate_up, w_out, w_ff1, w_ff2 = (
        w.astype(BF16) for w in (w_in, w_branch, w_gate_up, w_out, w_ff1, w_ff2))
    xf = x.reshape(t, d)
    for i in range(depth):
        proj = _mm_norm(xf, norm1_g[i], w_in[i], relu2=False, tm=512, tn=768)
        gq_a, gk_a = qn_a[i].reshape(1, -1), kn_a[i].reshape(1, -1)
        parts = [_attn_a_group(proj, gq_a, gk_a, g, batch) for g in range(len(A_GROUPS))]
        o_a = _mix_a([p[0] for p in parts], [p[1] for p in parts], tm=512)
        o_b = _attn_b(proj, jnp.tile(qn_b[i], 2).reshape(1, -1), jnp.tile(kn_b[i], 2).reshape(1, -1),
                      sinks[i], batch)
        o_c = _retention(proj, cos, sa, sb, batch)
        mixed = _merge(o_a, o_b, o_c, proj, w_branch[i], w_gate_up[i], b_gate[i].reshape(1, -1),
                       tm=512, tn=1024)
        xf = _mm_res(mixed, w_out[i], xf, tm=1024, tn=1024, tk=1024)
        hid = _mm_norm(xf, norm2_g[i], w_ff1[i], relu2=True, tm=512, tn=1024)
        xf = _mm_res(hid, w_ff2[i], xf, tm=1024, tn=1024, tk=1024)
    return xf.reshape(batch, s, d)
```
